```python
import math
import jax, jax.numpy as jnp
from jax import lax
import numpy as np

D_MODEL = 1024
BATCH = 2
SEQ = 16384
DEPTH = 1

N_FOURIER_GROUPS = 4
FOURIER_GROUP_DIM = 128
FOURIER_WIDTH = N_FOURIER_GROUPS * FOURIER_GROUP_DIM
DIL_PATTERNS = ((128, 1), (512, 4), (2048, 16))
N_DIL_GROUPS = 3
HEADS_PER_DIL_GROUP = 4
DIL_HEAD_DIM = 64
DIL_HEADS = N_DIL_GROUPS * HEADS_PER_DIL_GROUP
DIL_WIDTH = DIL_HEADS * DIL_HEAD_DIM
DIL_OUT_WIDTH = HEADS_PER_DIL_GROUP * DIL_HEAD_DIM
MEM_LEN = 256
MEM_HEADS = 4
MEM_HEAD_DIM = 128
MEM_WIDTH = MEM_HEADS * MEM_HEAD_DIM
N_BRANCHES = 3
COL_A_END = FOURIER_WIDTH
COL_B_END = COL_A_END + 3 * DIL_WIDTH
COL_C_END = COL_B_END + MEM_WIDTH
COL_TOTAL = COL_C_END + N_BRANCHES * D_MODEL
REL_BUCKETS = 32
REL_MAX_DIST = 1024
PEER_HEADS = 8
PEER_KEYS = 128
PEER_EXPERTS = PEER_KEYS * PEER_KEYS
PEER_QUERY_DIM = 128
PEER_HALF = PEER_QUERY_DIM // 2
PEER_TOPK = 16
PEER_CHUNK = 128
ALPHA = (2.0 * DEPTH) ** 0.25
BETA = (8.0 * DEPTH) ** -0.25
LN_EPS = 1e-5
NEG_INF = -1e30

kernel_name = "hybrid_fnet_dilated_memattn_peer_deepnorm"


def layer_norm(x, g, b):
    xf = x.astype(jnp.float32)
    mu = jnp.mean(xf, -1, keepdims=True)
    var = jnp.mean(jnp.square(xf - mu), -1, keepdims=True)
    y = (xf - mu) * lax.rsqrt(var + LN_EPS) * g.astype(jnp.float32) + b.astype(jnp.float32)
    return y.astype(x.dtype)


def t5_bucket(rel):
    half = REL_BUCKETS // 2
    max_exact = half // 2
    ret = jnp.where(rel > 0, half, 0)
    n = jnp.abs(rel)
    nf = jnp.maximum(n, 1).astype(jnp.float32)
    large = max_exact + (jnp.log(nf / max_exact) / math.log(REL_MAX_DIST / max_exact)
                         * (half - max_exact)).astype(jnp.int32)
    large = jnp.minimum(large, half - 1)
    return ret + jnp.where(n < max_exact, n, large)


def fourier_mix(u):
    f = jnp.fft.fftn(u.astype(jnp.float32), axes=(1, 3), norm="ortho")
    return jnp.real(f).astype(u.dtype)


def dilated_window_attention(q, k, v, bias_table, window, dilation):
    bsz, seq, heads, hd = q.shape
    steps = window // (2 * dilation)
    blk = steps
    sub_len = seq // dilation
    n_blk = -(-sub_len // blk)
    pad_len = n_blk * blk

    def sub(t):
        return t.reshape(bsz, sub_len, dilation, heads, hd)

    qs = jnp.pad(sub(q), ((0, 0), (0, pad_len - sub_len), (0, 0), (0, 0), (0, 0)))
    qs = qs.reshape(bsz, n_blk, blk, dilation, heads, hd)

    def windows(t):
        tp = jnp.pad(sub(t), ((0, 0), (blk, pad_len - sub_len + blk), (0, 0), (0, 0), (0, 0)))
        tp = tp.reshape(bsz, n_blk + 2, blk, dilation, heads, hd)
        return jnp.concatenate([tp[:, :-2], tp[:, 1:-1], tp[:, 2:]], axis=2)

    kw, vw = windows(k), windows(v)
    s = jnp.einsum('bnqrhc,bnkrhc->bnrhqk', qs, kw,
                   preferred_element_type=jnp.float32) * (hd ** -0.5)
    step_off = jnp.arange(3 * blk)[None, :] - blk - jnp.arange(blk)[:, None]
    band = jnp.abs(step_off) <= steps
    key_step = jnp.arange(n_blk)[:, None] * blk + jnp.arange(3 * blk)[None, :] - blk
    inside = (key_step >= 0) & (key_step < sub_len)
    mask = band[None] & inside[:, None, :]
    bias = jnp.transpose(bias_table[t5_bucket(step_off * dilation)], (2, 0, 1)).astype(jnp.float32)
    s = jnp.where(mask[None, :, None, None], s + bias[None, None, None], NEG_INF)
    m = jnp.max(s, -1, keepdims=True)
    p = jnp.exp(s - m)
    den = jnp.sum(p, -1)
    o = jnp.einsum('bnrhqk,bnkrhc->bnqrhc', p, vw.astype(jnp.float32))
    o = o / jnp.transpose(den, (0, 1, 4, 2, 3))[..., None]
    lse = jnp.transpose(m[..., 0] + jnp.log(den), (0, 1, 4, 2, 3))
    o = o.reshape(bsz, pad_len, dilation, heads, hd)[:, :sub_len].reshape(bsz, seq, heads, hd)
    lse = lse.reshape(bsz, pad_len, dilation, heads)[:, :sub_len].reshape(bsz, seq, heads)
    return o, lse


def mixing_sublayer(h, mem, rel_bias, w_in, b_in, w_a, w_b, w_c, w_mem_kv, w_o, ln_g, ln_b):
    bsz, seq, d = h.shape
    z = h @ w_in + b_in
    za = z[..., :COL_A_END].reshape(bsz, seq, N_FOURIER_GROUPS, FOURIER_GROUP_DIM)
    y_a = fourier_mix(za).reshape(bsz, seq, FOURIER_WIDTH) @ w_a
    qkv = z[..., COL_A_END:COL_B_END].reshape(bsz, seq, 3, N_DIL_GROUPS, HEADS_PER_DIL_GROUP, DIL_HEAD_DIM)
    outs, lses = [], []
    for g, (win, dil) in enumerate(DIL_PATTERNS):
        o, lse = dilated_window_attention(
            qkv[:, :, 0, g], qkv[:, :, 1, g], qkv[:, :, 2, g],
            rel_bias[:, g * HEADS_PER_DIL_GROUP:(g + 1) * HEADS_PER_DIL_GROUP], win, dil)
        outs.append(o)
        lses.append(lse)
    wts = jax.nn.softmax(jnp.stack(lses), axis=0)
    o_b = jnp.einsum('gbsh,gbshc->bshc', wts, jnp.stack(outs)).astype(h.dtype)
    y_b = o_b.reshape(bsz, seq, DIL_OUT_WIDTH) @ w_b
    q_c = z[..., COL_B_END:COL_C_END].reshape(bsz, seq, MEM_HEADS, MEM_HEAD_DIM)
    kv = (mem @ w_mem_kv).reshape(bsz, mem.shape[1], 2, MEM_HEADS, MEM_HEAD_DIM)
    sc = jnp.einsum('bshc,bmhc->bhsm', q_c, kv[:, :, 0],
                    preferred_element_type=jnp.float32) * (MEM_HEAD_DIM ** -0.5)
    pc = jax.nn.softmax(sc, axis=-1).astype(h.dtype)
    y_c = jnp.einsum('bhsm,bmhc->bshc', pc, kv[:, :, 1]).reshape(bsz, seq, MEM_WIDTH) @ w_c
    gates = jax.nn.sigmoid(z[..., COL_C_END:].reshape(bsz, seq, N_BRANCHES, d))
    merged = gates[:, :, 0] * y_a + gates[:, :, 1] * y_b + gates[:, :, 2] * y_c
    return layer_norm(ALPHA * h + merged @ w_o, ln_g, ln_b)


def peer_sublayer(h, w_pq, sub_keys1, sub_keys2, u_tab, v_tab, ln_g, ln_b):
    bsz, seq, d = h.shape
    n_tok = bsz * seq
    xt = h.reshape(n_tok, d)
    q = (xt @ w_pq).reshape(n_tok, PEER_HEADS, 2, PEER_HALF).astype(jnp.float32)
    s1 = jnp.einsum('thc,kc->thk', q[:, :, 0], sub_keys1.astype(jnp.float32))
    s2 = jnp.einsum('thc,kc->thk', q[:, :, 1], sub_keys2.astype(jnp.float32))
    v1, i1 = lax.top_k(s1, PEER_TOPK)
    v2, i2 = lax.top_k(s2, PEER_TOPK)
    cand = (v1[..., :, None] + v2[..., None, :]).reshape(n_tok, PEER_HEADS, PEER_TOPK * PEER_TOPK)
    sc, ic = lax.top_k(cand, PEER_TOPK)
    e1 = jnp.take_along_axis(i1, ic // PEER_TOPK, axis=-1)
    e2 = jnp.take_along_axis(i2, ic % PEER_TOPK, axis=-1)
    experts = (e1 * PEER_KEYS + e2).reshape(n_tok, PEER_HEADS * PEER_TOPK)
    gate = jax.nn.softmax(sc, axis=-1).reshape(n_tok, PEER_HEADS * PEER_TOPK).astype(h.dtype)
    n_chunks = n_tok // PEER_CHUNK

    def chunk_fn(args):
        xc, ec, gc = args
        act = jax.nn.gelu(jnp.einsum('tkd,td->tk', u_tab[ec], xc), approximate=False)
        return jnp.einsum('tk,tkd->td', gc * act, v_tab[ec])

    y = lax.map(chunk_fn, (xt.reshape(n_chunks, PEER_CHUNK, d),
                           experts.reshape(n_chunks, PEER_CHUNK, -1),
                           gate.reshape(n_chunks, PEER_CHUNK, -1)))
    y = y.reshape(bsz, seq, d)
    return layer_norm(ALPHA * h + y, ln_g, ln_b)


def setup_inputs(seed: int = 0) -> dict:
    key = jax.random.key(seed)
    ks = jax.random.split(key, 24)
    f32 = jnp.float32

    def nrm(k, shape, scale):
        return jax.random.normal(k, shape, f32) * scale

    return {
        "x": nrm(ks[0], (BATCH, SEQ, D_MODEL), 1.0),
        "mem": nrm(ks[1], (BATCH, MEM_LEN, D_MODEL), 1.0),
        "rel_bias": nrm(ks[2], (REL_BUCKETS, DIL_HEADS), 0.5),
        "ln0_g": 1.0 + nrm(ks[3], (D_MODEL,), 0.02),
        "ln0_b": nrm(ks[4], (D_MODEL,), 0.02),
        "w_in": nrm(ks[5], (DEPTH, D_MODEL, COL_TOTAL), D_MODEL ** -0.5),
        "b_in": nrm(ks[6], (DEPTH, COL_TOTAL), 0.02),
        "w_a": nrm(ks[7], (DEPTH, FOURIER_WIDTH, D_MODEL), FOURIER_WIDTH ** -0.5),
        "w_b": nrm(ks[8], (DEPTH, DIL_OUT_WIDTH, D_MODEL), DIL_OUT_WIDTH ** -0.5),
        "w_c": nrm(ks[9], (DEPTH, MEM_WIDTH, D_MODEL), MEM_WIDTH ** -0.5),
        "w_mem_kv": nrm(ks[10], (DEPTH, D_MODEL, 2 * MEM_WIDTH), D_MODEL ** -0.5),
        "w_o": nrm(ks[11], (DEPTH, D_MODEL, D_MODEL), BETA * D_MODEL ** -0.5),
        "ln1_g": 1.0 + nrm(ks[12], (DEPTH, D_MODEL), 0.02),
        "ln1_b": nrm(ks[13], (DEPTH, D_MODEL), 0.02),
        "w_pq": nrm(ks[14], (DEPTH, D_MODEL, PEER_HEADS * PEER_QUERY_DIM), D_MODEL ** -0.5),
        "sub_keys1": nrm(ks[15], (DEPTH, PEER_KEYS, PEER_HALF), PEER_HALF ** -0.5),
        "sub_keys2": nrm(ks[16], (DEPTH, PEER_KEYS, PEER_HALF), PEER_HALF ** -0.5),
        "u_tab": nrm(ks[17], (DEPTH, PEER_EXPERTS, D_MODEL), D_MODEL ** -0.5),
        "v_tab": nrm(ks[18], (DEPTH, PEER_EXPERTS, D_MODEL), BETA),
        "ln2_g": 1.0 + nrm(ks[19], (DEPTH, D_MODEL), 0.02),
        "ln2_b": nrm(ks[20], (DEPTH, D_MODEL), 0.02),
    }


def reference(x, mem, rel_bias, ln0_g, ln0_b, w_in, b_in, w_a, w_b, w_c, w_mem_kv, w_o,
              ln1_g, ln1_b, w_pq, sub_keys1, sub_keys2, u_tab, v_tab, ln2_g, ln2_b):
    h = layer_norm(x, ln0_g, ln0_b)
    for l in range(DEPTH):
        h = mixing_sublayer(h, mem, rel_bias, w_in[l], b_in[l], w_a[l], w_b[l], w_c[l],
                            w_mem_kv[l], w_o[l], ln1_g[l], ln1_b[l])
        h = peer_sublayer(h, w_pq[l], sub_keys1[l], sub_keys2[l], u_tab[l], v_tab[l],
                          ln2_g[l], ln2_b[l])
    return h
```

```python
import functools
import math

import jax
import jax.numpy as jnp
import numpy as np
from jax import lax
from jax.experimental import pallas as pl
from jax.experimental.pallas import tpu as pltpu

FOURIER_GROUP_DIM = 128
FOURIER_WIDTH = 512
DIL_PATTERNS = ((128, 1), (512, 4), (2048, 16))
HEADS_PER_DIL_GROUP = 4
DIL_HEAD_DIM = 64
DIL_GROUP_WIDTH = HEADS_PER_DIL_GROUP * DIL_HEAD_DIM
DIL_WIDTH = 768
DIL_STEPS = 64
MEM_HEADS = 4
MEM_HEAD_DIM = 128
MEM_WIDTH = 512
COL_A_END = FOURIER_WIDTH
COL_B_END = COL_A_END + 3 * DIL_WIDTH
COL_C_END = COL_B_END + MEM_WIDTH
REL_BUCKETS = 32
REL_MAX_DIST = 1024
PEER_HEADS = 8
PEER_KEYS = 128
PEER_HALF = 64
PEER_TOPK = 16
PEER_PICKS = PEER_HEADS * PEER_TOPK
DEPTH = 1
ALPHA = (2.0 * DEPTH) ** 0.25
LN_EPS = 1e-5
NEG_INF = -1e30

LANES = 128
VMEM_LIMIT_BYTES = 56 * 1024 * 1024

BF16 = jnp.bfloat16
F32 = jnp.float32


def _params(sem):
    return pltpu.CompilerParams(dimension_semantics=sem, vmem_limit_bytes=VMEM_LIMIT_BYTES)


def _layer_norm(x, g, b):
    mu = jnp.mean(x, axis=-1, keepdims=True)
    xc = x - mu
    var = jnp.mean(xc * xc, axis=-1, keepdims=True)
    return xc * lax.rsqrt(var + LN_EPS) * g + b


def _ln_proj_body(x_ref, g_ref, b_ref, w_ref, bias_ref, h_ref, za_ref, qkv_ref, qc_ref, gate_ref):
    h = _layer_norm(x_ref[...], g_ref[...], b_ref[...])
    h_ref[...] = h
    hb = h.astype(BF16)

    def proj(c0, c1):
        return jnp.dot(hb, w_ref[:, c0:c1], preferred_element_type=F32) + bias_ref[:, c0:c1]

    chunk = 512
    for c0 in range(0, COL_A_END, chunk):
        za_ref[:, c0:c0 + chunk] = proj(c0, c0 + chunk)
    for c0 in range(COL_A_END, COL_B_END, 768):
        qkv_ref[:, c0 - COL_A_END:c0 - COL_A_END + 768] = proj(c0, c0 + 768).astype(BF16)
    qc_ref[...] = proj(COL_B_END, COL_C_END).astype(BF16)
    n_gate = gate_ref.shape[1]
    for c0 in range(0, n_gate, chunk):
        gate_ref[:, c0:c0 + chunk] = proj(COL_C_END + c0, COL_C_END + c0 + chunk)


def _ln_proj(x2, g, b, w_bf, bias, tm=256):
    n, d = x2.shape
    cols = w_bf.shape[1]
    n_gate = cols - COL_C_END
    row = lambda w: pl.BlockSpec((tm, w), lambda i: (i, 0))
    const = lambda r, w: pl.BlockSpec((r, w), lambda i: (0, 0))
    return pl.pallas_call(
        _ln_proj_body,
        grid=(n // tm,),
        in_specs=[row(d), const(1, d), const(1, d), const(d, cols), const(1, cols)],
        out_specs=[row(d), row(FOURIER_WIDTH), row(3 * DIL_WIDTH), row(MEM_WIDTH), row(n_gate)],
        out_shape=[
            jax.ShapeDtypeStruct((n, d), F32),
            jax.ShapeDtypeStruct((n, FOURIER_WIDTH), F32),
            jax.ShapeDtypeStruct((n, 3 * DIL_WIDTH), BF16),
            jax.ShapeDtypeStruct((n, MEM_WIDTH), BF16),
            jax.ShapeDtypeStruct((n, n_gate), F32),
        ],
        compiler_params=_params(("parallel",)),
        name="ln_proj",
    )(x2, g, b, w_bf, bias)


def _fourier1_body(x_ref, f_ref, tc_ref, ts_ref, br_ref, bi_ref):
    n1 = x_ref.shape[2]
    for j in range(x_ref.shape[1]):
        xb = x_ref[0, j].astype(BF16)
        a = jnp.dot(f_ref[...], xb, preferred_element_type=F32)
        cx, sx = a[:n1], a[n1:]
        tc, ts = tc_ref[j], ts_ref[j]
        br_ref[0, j] = cx * tc - sx * ts
        bi_ref[0, j] = -(sx * tc) - cx * ts


def _fourier1(xt, f1, tcos, tsin, nb=8):
    bsz, n2, n1, w = xt.shape
    blk = pl.BlockSpec((1, nb, n1, w), lambda b, j: (b, j, 0, 0))
    tw = pl.BlockSpec((nb, n1, 1), lambda b, j: (j, 0, 0))
    out = jax.ShapeDtypeStruct(xt.shape, F32)
    return pl.pallas_call(
        _fourier1_body,
        grid=(bsz, n2 // nb),
        in_specs=[blk, pl.BlockSpec((2 * n1, n1), lambda b, j: (0, 0)), tw, tw],
        out_specs=[blk, blk],
        out_shape=[out, out],
        compiler_params=_params(("parallel", "parallel")),
        name="fourier_step1",
    )(xt, f1, tcos, tsin)


def _fourier2_body(br_ref, bi_ref, fl_ref, fr_ref, cc_ref, sc_ref, y_ref):
    n2 = br_ref.shape[2]
    for j in range(br_ref.shape[1]):
        br = br_ref[0, j].astype(BF16)
        bi = bi_ref[0, j].astype(BF16)
        f = (jnp.dot(fl_ref[...], br, preferred_element_type=F32)
             + jnp.dot(fr_ref[...], bi, preferred_element_type=F32))
        fre = f[:n2].astype(BF16)
        fim = f[n2:].astype(BF16)
        for g in range(FOURIER_WIDTH // FOURIER_GROUP_DIM):
            sl = slice(g * FOURIER_GROUP_DIM, (g + 1) * FOURIER_GROUP_DIM)
            y_ref[0, j, :, sl] = (jnp.dot(fre[:, sl], cc_ref[...], preferred_element_type=F32)
                                  + jnp.dot(fim[:, sl], sc_ref[...], preferred_element_type=F32))


def _fourier2(brt, bit, f2l, f2r, cc, sc, nb=8):
    bsz, n1, n2, w = brt.shape
    blk = pl.BlockSpec((1, nb, n2, w), lambda b, j: (b, j, 0, 0))
    const = lambda a: pl.BlockSpec(a.shape, lambda b, j: (0, 0))
    return pl.pallas_call(
        _fourier2_body,
        grid=(bsz, n1 // nb),
        in_specs=[blk, blk, const(f2l), const(f2r), const(cc), const(sc)],
        out_specs=blk,
        out_shape=jax.ShapeDtypeStruct(brt.shape, F32),
        compiler_params=_params(("parallel", "parallel")),
        name="fourier_step2",
    )(brt, bit, f2l, f2r, cc, sc)


def _dft_mats(n):
    ang = 2.0 * np.pi * np.outer(np.arange(n), np.arange(n)) / n
    return np.cos(ang), np.sin(ang)


def _fourier_mix(za, seq):
    bsz = za.shape[0]
    n2 = LANES
    n1 = seq // n2
    c1, s1 = _dft_mats(n1)
    c2, s2 = _dft_mats(n2)
    cch, sch = _dft_mats(FOURIER_GROUP_DIM)
    f1 = jnp.asarray(np.concatenate([c1, s1], axis=0), BF16)
    f2l = jnp.asarray(np.concatenate([c2, -s2], axis=0), BF16)
    f2r = jnp.asarray(np.concatenate([s2, c2], axis=0), BF16)
    scale = 1.0 / math.sqrt(seq * FOURIER_GROUP_DIM)
    tang = 2.0 * np.pi * np.outer(np.arange(n2), np.arange(n1)) / seq
    tcos = jnp.asarray((np.cos(tang) * scale)[:, :, None], F32)
    tsin = jnp.asarray((np.sin(tang) * scale)[:, :, None], F32)
    xt = za.reshape(bsz, n1, n2, FOURIER_WIDTH).transpose(0, 2, 1, 3)
    br, bi = _fourier1(xt, f1, tcos, tsin, nb=8)
    brt = br.transpose(0, 2, 1, 3)
    bit = bi.transpose(0, 2, 1, 3)
    y = _fourier2(brt, bit, f2l, f2r, jnp.asarray(cch, BF16), jnp.asarray(sch, BF16),
                  nb=min(8, n1))
    return y.transpose(0, 2, 1, 3).reshape(bsz, seq, FOURIER_WIDTH)


def _dil_attn_body(q_ref, k_ref, v_ref, bias_ref, acc_ref, m_ref, l_ref, *, sub_len):
    tq = q_ref.shape[0]
    kw = tq + 2 * DIL_STEPS
    start = pl.multiple_of(pl.program_id(1) * tq, tq)
    kwin = k_ref[pl.ds(start, kw), :]
    vwin = v_ref[pl.ds(start, kw), :]
    kpos = start - DIL_STEPS + lax.broadcasted_iota(jnp.int32, (1, kw), 1)
    inside = (kpos >= 0) & (kpos < sub_len)
    for h in range(HEADS_PER_DIL_GROUP):
        sl = slice(h * DIL_HEAD_DIM, (h + 1) * DIL_HEAD_DIM)
        s = lax.dot_general(q_ref[:, sl], kwin[:, sl], (((1,), (1,)), ((), ())),
                            preferred_element_type=F32)
        s = s * (DIL_HEAD_DIM ** -0.5) + bias_ref[h]
        s = jnp.where(inside, s, NEG_INF)
        m = jnp.max(s, axis=-1, keepdims=True)
        p = jnp.exp(s - m)
        l = jnp.sum(p, axis=-1, keepdims=True)
        acc_ref[:, sl] = jnp.dot(p.astype(BF16), vwin[:, sl], preferred_element_type=F32)
        m_ref[:, sl] = jnp.broadcast_to(m, (tq, DIL_HEAD_DIM))
        l_ref[:, sl] = jnp.broadcast_to(l, (tq, DIL_HEAD_DIM))


def _dil_attn(q, k, v, bias, sub_len, tq):
    nseq = q.shape[0]
    kw = tq + 2 * DIL_STEPS
    qblk = pl.BlockSpec((None, tq, DIL_GROUP_WIDTH), lambda s, i: (s, i, 0))
    kvblk = pl.BlockSpec((None, sub_len + 2 * DIL_STEPS, DIL_GROUP_WIDTH), lambda s, i: (s, 0, 0))
    out = jax.ShapeDtypeStruct((nseq, sub_len, DIL_GROUP_WIDTH), F32)
    return pl.pallas_call(
        functools.partial(_dil_attn_body, sub_len=sub_len),
        grid=(nseq, sub_len // tq),
        in_specs=[qblk, kvblk, kvblk,
                  pl.BlockSpec((HEADS_PER_DIL_GROUP, tq, kw), lambda s, i: (0, 0, 0))],
        out_specs=[qblk, qblk, qblk],
        out_shape=[out, out, out],
        compiler_params=_params(("parallel", "arbitrary")),
        name="dilated_attention",
    )(q, k, v, bias)


def _t5_bucket(rel):
    half = REL_BUCKETS // 2
    max_exact = half // 2
    ret = jnp.where(rel > 0, half, 0)
    n = jnp.abs(rel)
    nf = jnp.maximum(n, 1).astype(F32)
    large = max_exact + (jnp.log(nf / max_exact) / math.log(REL_MAX_DIST / max_exact)
                         * (half - max_exact)).astype(jnp.int32)
    large = jnp.minimum(large, half - 1)
    return ret + jnp.where(n < max_exact, n, large)


def _dil_bias(rel_bias_g, dilation, tq):
    step_off = (np.arange(tq + 2 * DIL_STEPS)[None, :] - DIL_STEPS - np.arange(tq)[:, None]).astype(np.int32)
    band = jnp.asarray(np.abs(step_off) <= DIL_STEPS)
    bias = jnp.transpose(rel_bias_g[_t5_bucket(jnp.asarray(step_off) * dilation)], (2, 0, 1)).astype(F32)
    return jnp.where(band[None], bias, NEG_INF)


def _dilated_branch(qkv, rel_bias, bsz, seq):
    outs = []
    for g, (_, dil) in enumerate(DIL_PATTERNS):
        sub_len = seq // dil
        tq = min(256, sub_len)

        def split(t, pad):
            c0 = t * DIL_WIDTH + g * DIL_GROUP_WIDTH
            a = qkv[:, c0:c0 + DIL_GROUP_WIDTH].reshape(bsz, sub_len, dil, DIL_GROUP_WIDTH)
            a = a.transpose(0, 2, 1, 3).reshape(bsz * dil, sub_len, DIL_GROUP_WIDTH)
            if pad:
                a = jnp.pad(a, ((0, 0), (DIL_STEPS, DIL_STEPS), (0, 0)))
            return a

        bias = _dil_bias(rel_bias[:, g * HEADS_PER_DIL_GROUP:(g + 1) * HEADS_PER_DIL_GROUP], dil, tq)
        res = _dil_attn(split(0, False), split(1, True), split(2, True), bias, sub_len, tq)

        def merge(a):
            a = a.reshape(bsz, dil, sub_len, DIL_GROUP_WIDTH).transpose(0, 2, 1, 3)
            return a.reshape(bsz * seq, DIL_GROUP_WIDTH)

        outs.extend(merge(a) for a in res)
    return outs


def _matmul_body(a_ref, w_ref, o_ref):
    o_ref[...] = jnp.dot(a_ref[...].astype(BF16), w_ref[...], preferred_element_type=F32).astype(o_ref.dtype)


def _mem_kv(mem2, w_bf):
    m, d = mem2.shape
    n = w_bf.shape[1]
    return pl.pallas_call(
        _matmul_body,
        grid=(1,),
        in_specs=[pl.BlockSpec((m, d), lambda i: (0, 0)), pl.BlockSpec((d, n), lambda i: (0, 0))],
        out_specs=pl.BlockSpec((m, n), lambda i: (0, 0)),
        out_shape=jax.ShapeDtypeStruct((m, n), BF16),
        compiler_params=_params(("arbitrary",)),
        name="mem_kv_proj",
    )(mem2, w_bf)


def _merge_body(h_ref, ya_ref, d0a, d0m, d0l, d1a, d1m, d1l, d2a, d2m, d2l, qc_ref, kv_ref, gate_ref,
                wa_ref, wb_ref, wc_ref, wo_ref, g_ref, b_ref, o_ref):
    d = h_ref.shape[1]
    y_a = jnp.dot(ya_ref[...].astype(BF16), wa_ref[...], preferred_element_type=F32)
    ms = (d0m[...], d1m[...], d2m[...])
    mx = jnp.maximum(jnp.maximum(ms[0], ms[1]), ms[2])
    es = [jnp.exp(m - mx) for m in ms]
    num = es[0] * d0a[...] + es[1] * d1a[...] + es[2] * d2a[...]
    den = es[0] * d0l[...] + es[1] * d1l[...] + es[2] * d2l[...]
    o_b = num / den
    y_b = jnp.dot(o_b.astype(BF16), wb_ref[...], preferred_element_type=F32)
    heads = []
    for hd in range(MEM_HEADS):
        sl = slice(hd * MEM_HEAD_DIM, (hd + 1) * MEM_HEAD_DIM)
        vsl = slice(MEM_WIDTH + hd * MEM_HEAD_DIM, MEM_WIDTH + (hd + 1) * MEM_HEAD_DIM)
        s = lax.dot_general(qc_ref[:, sl], kv_ref[:, sl], (((1,), (1,)), ((), ())),
                            preferred_element_type=F32) * (MEM_HEAD_DIM ** -0.5)
        p = jnp.exp(s - jnp.max(s, axis=-1, keepdims=True))
        p = p / jnp.sum(p, axis=-1, keepdims=True)
        heads.append(jnp.dot(p.astype(BF16), kv_ref[:, vsl], preferred_element_type=F32))
    o_c = jnp.concatenate(heads, axis=-1)
    y_c = jnp.dot(o_c.astype(BF16), wc_ref[...], preferred_element_type=F32)
    merged = (jax.nn.sigmoid(gate_ref[:, 0:d]) * y_a
              + jax.nn.sigmoid(gate_ref[:, d:2 * d]) * y_b
              + jax.nn.sigmoid(gate_ref[:, 2 * d:3 * d]) * y_c)
    out = jnp.dot(merged.astype(BF16), wo_ref[...], preferred_element_type=F32)
    o_ref[...] = _layer_norm(ALPHA * h_ref[...] + out, g_ref[...], b_ref[...])


def _merge(h0, ya, dil_parts, qc, kv, gates, wa, wb, wc, wo, g, b, seq, tm=256):
    n, d = h0.shape
    tiles_per_batch = seq // tm
    row = lambda w: pl.BlockSpec((tm, w), lambda i: (i, 0))
    const = lambda a: pl.BlockSpec(a.shape, lambda i: (0,) * a.ndim)
    kvspec = pl.BlockSpec((None,) + kv.shape[1:], lambda i: (i // tiles_per_batch, 0, 0))
    return pl.pallas_call(
        _merge_body,
        grid=(n // tm,),
        in_specs=[row(d), row(FOURIER_WIDTH)] + [row(DIL_GROUP_WIDTH)] * 9
                 + [row(MEM_WIDTH), kvspec, row(3 * d), const(wa), const(wb), const(wc), const(wo),
                    const(g), const(b)],
        out_specs=row(d),
        out_shape=jax.ShapeDtypeStruct((n, d), F32),
        compiler_params=_params(("parallel",)),
        name="merge_ln1",
    )(h0, ya, *dil_parts, qc, kv, gates, wa, wb, wc, wo, g, b)


def _topk_rows(vals, payload, k):
    rows = vals.shape[0]
    iota = lax.broadcasted_iota(jnp.int32, vals.shape, 0)
    out_v, out_p = [], []
    work = vals
    for _ in range(k):
        mx = jnp.max(work, axis=0, keepdims=True)
        first = jnp.min(jnp.where(work == mx, iota, rows), axis=0, keepdims=True)
        hit = iota == first
        out_v.append(mx)
        out_p.append(jnp.sum(jnp.where(hit, payload, 0), axis=0, keepdims=True))
        work = jnp.where(hit, -jnp.inf, work)
    return jnp.concatenate(out_v, axis=0), jnp.concatenate(out_p, axis=0)


def _peer_topk_body(h_ref, wq_ref, k1_ref, k2_ref, e_ref, gate_ref):
    tm = h_ref.shape[0]
    q = jnp.dot(h_ref[...].astype(BF16), wq_ref[...], preferred_element_type=F32)
    qb = q.astype(BF16)
    key_iota = lax.broadcasted_iota(jnp.int32, (PEER_KEYS, tm), 0)
    nt = (((1,), (1,)), ((), ()))
    for hd in range(PEER_HEADS):
        c0 = hd * 2 * PEER_HALF
        s1 = lax.dot_general(k1_ref[...], qb[:, c0:c0 + PEER_HALF], nt, preferred_element_type=F32)
        s2 = lax.dot_general(k2_ref[...], qb[:, c0 + PEER_HALF:c0 + 2 * PEER_HALF], nt,
                             preferred_element_type=F32)
        v1, i1 = _topk_rows(s1, key_iota, PEER_TOPK)
        v2, i2 = _topk_rows(s2, key_iota, PEER_TOPK)
        cand = jnp.concatenate([v1[a:a + 1] + v2 for a in range(PEER_TOPK)], axis=0)
        cid = jnp.concatenate([i1[a:a + 1] * PEER_KEYS + i2 for a in range(PEER_TOPK)], axis=0)
        sc, ex = _topk_rows(cand, cid, PEER_TOPK)
        p = jnp.exp(sc - sc[0:1])
        gate = p / jnp.sum(p, axis=0, keepdims=True)
        e_ref[hd * PEER_TOPK:(hd + 1) * PEER_TOPK, :] = ex
        gate_ref[hd * PEER_TOPK:(hd + 1) * PEER_TOPK, :] = gate


def _peer_topk(h1, wq, k1, k2, tm=256):
    n, d = h1.shape
    const = lambda a: pl.BlockSpec(a.shape, lambda i: (0, 0))
    oblk = pl.BlockSpec((PEER_PICKS, tm), lambda i: (0, i))
    return pl.pallas_call(
        _peer_topk_body,
        grid=(n // tm,),
        in_specs=[pl.BlockSpec((tm, d), lambda i: (i, 0)), const(wq), const(k1), const(k2)],
        out_specs=[oblk, oblk],
        out_shape=[jax.ShapeDtypeStruct((PEER_PICKS, n), jnp.int32),
                   jax.ShapeDtypeStruct((PEER_PICKS, n), F32)],
        compiler_params=_params(("parallel",)),
        name="peer_topk",
    )(h1, wq, k1, k2)


def _peer_gather_body(ids_cur, ids_nxt, x_ref, gate_ref, g_ref, b_ref, uv_ref, o_ref, buf, sem, y_scr):
    tt, d = x_ref.shape
    rows = tt * PEER_PICKS
    i = pl.program_id(0)
    n_steps = pl.num_programs(0)
    slot = i % 2

    def row_copy(ids, r, s):
        return pltpu.make_async_copy(uv_ref.at[pl.ds(ids[0, 0, r], 1), :], buf.at[s, pl.ds(r, 1), :], sem.at[s])

    def issue(ids, s):
        unroll = 8

        def step(j, c):
            for u in range(unroll):
                row_copy(ids, j * unroll + u, s).start()
            return c

        lax.fori_loop(0, rows // unroll, step, 0)

    @pl.when(i == 0)
    def _():
        issue(ids_cur, 0)

    @pl.when(i + 1 < n_steps)
    def _():
        issue(ids_nxt, 1 - slot)

    pltpu.make_async_copy(uv_ref.at[pl.ds(0, rows), :], buf.at[slot], sem.at[slot]).wait()

    lane = lax.broadcasted_iota(jnp.int32, (PEER_PICKS, LANES), 1)
    s_cols = jnp.zeros((PEER_PICKS, LANES), F32)
    for t in range(tt):
        u = buf[slot, t * PEER_PICKS:(t + 1) * PEER_PICKS, 0:d]
        s = jnp.sum(u * x_ref[t:t + 1, :], axis=1, keepdims=True)
        s_cols = jnp.where(lane == t, s, s_cols)
    act = 0.5 * s_cols * (1.0 + lax.erf(s_cols * (2.0 ** -0.5)))
    w = act[:, 0:tt] * gate_ref[...]
    for t in range(tt):
        v = buf[slot, t * PEER_PICKS:(t + 1) * PEER_PICKS, d:2 * d]
        y_scr[t:t + 1, :] = jnp.sum(v * w[:, t:t + 1], axis=0, keepdims=True)
    o_ref[...] = _layer_norm(ALPHA * x_ref[...] + y_scr[...], g_ref[...], b_ref[...])


def _peer_gather(h1, experts_t, gates_t, uv, g, b, tt=16):
    n, d = h1.shape
    steps = n // tt
    rows = tt * PEER_PICKS
    ids = experts_t.T.reshape(steps, 1, rows)
    gates = gates_t.reshape(PEER_PICKS, steps, tt).transpose(1, 0, 2)
    ids_spec = lambda f: pl.BlockSpec((1, 1, rows), f, memory_space=pltpu.SMEM)
    const = lambda a: pl.BlockSpec(a.shape, lambda i: (0, 0))
    return pl.pallas_call(
        _peer_gather_body,
        grid=(steps,),
        in_specs=[ids_spec(lambda i: (i, 0, 0)),
                  ids_spec(lambda i: (jnp.minimum(i + 1, steps - 1), 0, 0)),
                  pl.BlockSpec((tt, d), lambda i: (i, 0)),
                  pl.BlockSpec((None, PEER_PICKS, tt), lambda i: (i, 0, 0)),
                  const(g), const(b),
                  pl.BlockSpec(memory_space=pl.ANY)],
        out_specs=pl.BlockSpec((tt, d), lambda i: (i, 0)),
        out_shape=jax.ShapeDtypeStruct((n, d), F32),
        scratch_shapes=[pltpu.VMEM((2, rows, 2 * d), F32), pltpu.SemaphoreType.DMA((2,)),
                        pltpu.VMEM((tt, d), F32)],
        compiler_params=_params(("arbitrary",)),
        name="peer_experts",
    )(ids, ids, h1, gates, g, b, uv)


def _mixing_tail(h0, za, qkv, qc, gates, mem, rel_bias, w_a, w_b, w_c, w_mem_kv, w_o, ln_g, ln_b, bsz, seq):
    n, d = h0.shape
    row = lambda a: a.reshape(1, -1)
    ya = _fourier_mix(za.reshape(bsz, seq, FOURIER_WIDTH), seq).reshape(n, FOURIER_WIDTH)
    dil_parts = _dilated_branch(qkv, rel_bias, bsz, seq)
    kv = _mem_kv(mem.reshape(bsz * mem.shape[1], d), w_mem_kv.astype(BF16))
    kv = kv.reshape(bsz, mem.shape[1], 2 * MEM_WIDTH)
    return _merge(h0, ya, dil_parts, qc, kv, gates, w_a.astype(BF16), w_b.astype(BF16),
                  w_c.astype(BF16), w_o.astype(BF16), row(ln_g), row(ln_b), seq)


def _peer(h1, w_pq, sub_keys1, sub_keys2, u_tab, v_tab, ln_g, ln_b):
    row = lambda a: a.reshape(1, -1)
    experts_t, gates_t = _peer_topk(h1, w_pq.astype(BF16), sub_keys1.astype(BF16), sub_keys2.astype(BF16))
    uv = jnp.concatenate([u_tab, v_tab], axis=1)
    return _peer_gather(h1, experts_t, gates_t, uv, row(ln_g), row(ln_b))


def kernel(x, mem, rel_bias, ln0_g, ln0_b, w_in, b_in, w_a, w_b, w_c, w_mem_kv, w_o, ln1_g, ln1_b,
           w_pq, sub_keys1, sub_keys2, u_tab, v_tab, ln2_g, ln2_b):
    bsz, seq, d = x.shape
    n = bsz * seq
    row = lambda a: a.reshape(1, -1)
    assert w_in.shape[0] == DEPTH
    l = 0
    h0, za, qkv, qc, gates = _ln_proj(x.reshape(n, d), row(ln0_g), row(ln0_b), w_in[l].astype(BF16),
                                      row(b_in[l]))
    h1 = _mixing_tail(h0, za, qkv, qc, gates, mem, rel_bias, w_a[l], w_b[l], w_c[l], w_mem_kv[l], w_o[l],
                      ln1_g[l], ln1_b[l], bsz, seq)
    h2 = _peer(h1, w_pq[l], sub_keys1[l], sub_keys2[l], u_tab[l], v_tab[l], ln2_g[l], ln2_b[l])
    return h2.reshape(bsz, seq, d)
```
